```python
import math
import jax, jax.numpy as jnp
from jax import lax
import numpy as np

D_MODEL = 4096
BATCH = 4
SEQ = 2048
DEPTH = 1
DEC_BATCH = 128
DEC_SEQ = 4
PAST_LEN = 16384
PAGE_SIZE = 128

SSD_HEADS = 64
SSD_HEAD_DIM = 64
D_SSD = SSD_HEADS * SSD_HEAD_DIM
SSD_GROUPS = 8
SSD_STATE = 128
SSD_CONV = 4
SSD_CHUNK = 128
CONV_DIM = D_SSD + 2 * SSD_GROUPS * SSD_STATE
DT_MIN = 1e-3
DT_MAX = 1e-1
SG_CHUNK = 128
SG_GROUPS = 8
D_SG = 2048
SG_HEAD_DIM = D_SG // SG_GROUPS
IN_SPLITS = (D_SSD,
             D_SSD + CONV_DIM,
             D_SSD + CONV_DIM + SSD_HEADS,
             D_SSD + CONV_DIM + SSD_HEADS + D_SG,
             D_SSD + CONV_DIM + SSD_HEADS + 2 * D_SG,
             D_SSD + CONV_DIM + SSD_HEADS + 2 * D_SG + D_MODEL)
D_IN = D_SSD + CONV_DIM + SSD_HEADS + 2 * D_SG + 2 * D_MODEL
N_GROUPS = 8
EXPERTS_PER_GROUP = 4
N_EXPERTS = N_GROUPS * EXPERTS_PER_GROUP
D_EXPERT = 512
TOP_K = 2
ALPHA = (2 * DEPTH) ** 0.25
BETA = (8 * DEPTH) ** -0.25
LN_EPS = 1e-5
RMS_EPS = 1e-5

kernel_name = 'hybrid_ssd_sgu_hmoe_deepnorm_step'


def layer_norm(x, g, b):
    xf = x.astype(jnp.float32)
    mu = jnp.mean(xf, -1, keepdims=True)
    var = jnp.mean(jnp.square(xf - mu), -1, keepdims=True)
    return ((xf - mu) * lax.rsqrt(var + LN_EPS)).astype(x.dtype) * g + b


def grouped_rms_norm(y, g):
    b, l, d = y.shape
    yf = y.astype(jnp.float32).reshape(b, l, SSD_GROUPS, d // SSD_GROUPS)
    yf = yf * lax.rsqrt(jnp.mean(yf * yf, -1, keepdims=True) + RMS_EPS)
    return yf.reshape(b, l, d).astype(y.dtype) * g


def causal_conv(xbc, buf, w, bias):
    l = xbc.shape[1]
    xp = jnp.concatenate([buf.astype(xbc.dtype), xbc], axis=1)
    y = bias
    for k in range(SSD_CONV):
        y = y + w[k] * xp[:, k:k + l]
    return y, xp[:, -(SSD_CONV - 1):]


def ssd_chunked(x, dt, a, bm, cm, s0):
    f32 = jnp.float32
    b, l = x.shape[:2]
    q = min(SSD_CHUNK, l)
    pad = (-l) % q
    x, dt, bm, cm = (jnp.pad(t.astype(f32), [(0, 0), (0, pad)] + [(0, 0)] * (t.ndim - 2))
                     for t in (x, dt, bm, cm))
    nc = (l + pad) // q
    hpg = SSD_HEADS // SSD_GROUPS
    x = x.reshape(b, nc, q, SSD_GROUPS, hpg, SSD_HEAD_DIM)
    dt = dt.reshape(b, nc, q, SSD_GROUPS, hpg)
    bm = bm.reshape(b, nc, q, SSD_GROUPS, SSD_STATE)
    cm = cm.reshape(b, nc, q, SSD_GROUPS, SSD_STATE)
    a_cs = jnp.cumsum(dt * a.reshape(SSD_GROUPS, hpg), axis=2)
    causal = jnp.tril(jnp.ones((q, q), dtype=bool))[:, :, None, None]
    seg = a_cs[:, :, :, None] - a_cs[:, :, None, :]
    decay = jnp.exp(jnp.where(causal, seg, -jnp.inf))
    scores = jnp.einsum('bcign,bcjgn->bcijg', cm, bm)
    y_diag = jnp.einsum('bcijgk,bcjgkp->bcigkp', scores[..., None] * decay * dt[:, :, None], x)
    to_end = jnp.exp(a_cs[:, :, -1:] - a_cs) * dt
    states = jnp.einsum('bcjgn,bcjgkp->bcgkpn', bm, x * to_end[..., None])
    chunk_decay = jnp.exp(a_cs[:, :, -1])

    def step(s, inp):
        st, dc = inp
        return s * dc[..., None, None] + st, s

    s0 = s0.astype(f32).reshape(b, SSD_GROUPS, hpg, SSD_HEAD_DIM, SSD_STATE)
    s_fin, s_prev = lax.scan(step, s0, (jnp.moveaxis(states, 1, 0), jnp.moveaxis(chunk_decay, 1, 0)))
    s_prev = jnp.moveaxis(s_prev, 0, 1)
    y_off = jnp.einsum('bcign,bcgkpn->bcigkp', cm, s_prev) * jnp.exp(a_cs)[..., None]
    y = (y_diag + y_off).reshape(b, nc * q, SSD_HEADS, SSD_HEAD_DIM)[:, :l]
    return y, s_fin.reshape(b, SSD_HEADS, SSD_HEAD_DIM, SSD_STATE)


def spatial_gating(u, v, sg_w, sg_b):
    b, l = u.shape[:2]
    pad = (-l) % SG_CHUNK
    nc = (l + pad) // SG_CHUNK
    vp = jnp.pad(v, ((0, 0), (0, pad), (0, 0))).reshape(b, nc, SG_CHUNK, SG_GROUPS, SG_HEAD_DIM)
    mixed = jnp.einsum('gij,bcjgd->bcigd', jnp.tril(sg_w), vp) + sg_b.T[:, :, None]
    return u * mixed.reshape(b, nc * SG_CHUNK, D_SG)[:, :l]


def gated_mixer(x, conv_buf, ssm_state, p):
    b, l, _ = x.shape
    h = jnp.einsum('bld,de->ble', x, p['w_in'])
    z, xbc, dt_raw, u, v, gate_ssd, gate_sg = jnp.split(h, IN_SPLITS, axis=-1)
    xbc, new_buf = causal_conv(xbc, conv_buf, p['conv_w'], p['conv_b'])
    xbc = jax.nn.silu(xbc)
    xs, bm, cm = jnp.split(xbc, [D_SSD, D_SSD + SSD_GROUPS * SSD_STATE], axis=-1)
    xs = xs.reshape(b, l, SSD_HEADS, SSD_HEAD_DIM)
    dt = jax.nn.softplus(dt_raw.astype(jnp.float32) + p['dt_bias'].astype(jnp.float32))
    a = -jnp.exp(p['a_log'].astype(jnp.float32))
    y, new_state = ssd_chunked(xs, dt, a, bm.reshape(b, l, SSD_GROUPS, SSD_STATE),
                               cm.reshape(b, l, SSD_GROUPS, SSD_STATE), ssm_state)
    y = y.astype(x.dtype) + p['d_skip'][:, None] * xs
    y_ssd = grouped_rms_norm(y.reshape(b, l, D_SSD) * jax.nn.silu(z), p['ssd_norm_g'])
    u = jax.nn.gelu(u)
    v = layer_norm(jax.nn.gelu(v), p['sg_ln_g'], p['sg_ln_b'])
    y_sg = spatial_gating(u, v, p['sg_w'], p['sg_b'])
    merged = (jax.nn.sigmoid(gate_ssd) * jnp.einsum('ble,ed->bld', y_ssd, p['p_ssd'])
              + jax.nn.sigmoid(gate_sg) * jnp.einsum('ble,ed->bld', y_sg, p['p_sg']))
    out = jnp.einsum('bld,de->ble', merged, p['w_out'])
    return out, new_buf, new_state.astype(ssm_state.dtype), v


def hier_moe(x, p):
    b, l, d = x.shape
    xf = x.reshape(b * l, d)
    t = xf.shape[0]
    g_logits = jnp.einsum('td,dg->tg', xf, p['w_router_group']).astype(jnp.float32) + p['b_router_group']
    grp = jnp.argmax(g_logits, -1)
    p_grp = jnp.take_along_axis(jax.nn.softmax(g_logits, -1), grp[:, None], -1)
    e_logits = (jnp.einsum('td,de->te', xf, p['w_router_expert']).astype(jnp.float32)
                + p['b_router_expert']).reshape(t, N_GROUPS, EXPERTS_PER_GROUP)
    e_in = jnp.take_along_axis(e_logits, grp[:, None, None], 1)[:, 0]
    top_v, top_i = lax.top_k(e_in, TOP_K)
    top_w = jax.nn.softmax(top_v, -1) * p_grp
    expert_id = grp[:, None] * EXPERTS_PER_GROUP + top_i
    gates = jnp.sum(jax.nn.one_hot(expert_id, N_EXPERTS, dtype=jnp.float32) * top_w[..., None], axis=1)
    gates = gates.astype(x.dtype)
    y = jnp.zeros_like(xf)
    for g in range(N_GROUPS):
        sl = slice(g * EXPERTS_PER_GROUP, (g + 1) * EXPERTS_PER_GROUP)
        hg = jnp.einsum('td,edf->tef', xf, p['w_gate'][sl])
        hu = jnp.einsum('td,edf->tef', xf, p['w_up'][sl])
        hh = jax.nn.silu(hg) * hu * gates[:, sl, None]
        y = y + jnp.einsum('tef,efd->td', hh, p['w_down'][sl])
    return y.reshape(b, l, d)


def decoder_layer(x, conv_buf, ssm_state, p):
    m, new_buf, new_state, v = gated_mixer(x, conv_buf, ssm_state, p)
    x = layer_norm(ALPHA * x + m, p['ln1_g'], p['ln1_b'])
    x = layer_norm(ALPHA * x + hier_moe(x, p), p['ln2_g'], p['ln2_b'])
    return x, new_buf, new_state, v


def setup_inputs(seed: int = 0) -> dict:
    key = jax.random.key(seed)
    ks = jax.random.split(key, 32)
    f32 = jnp.float32
    L = DEPTH

    def nrm(i, shape, scale):
        return jax.random.normal(ks[i], shape, f32) * scale

    dt0 = jnp.exp(jax.random.uniform(ks[0], (L, SSD_HEADS), f32, math.log(DT_MIN), math.log(DT_MAX)))
    return {
        'x_prompt': nrm(1, (BATCH, SEQ, D_MODEL), 1.0),
        'x_sample': nrm(2, (DEC_BATCH, DEC_SEQ, D_MODEL), 1.0),
        'state_ssd': nrm(3, (L, DEC_BATCH, SSD_HEADS, SSD_HEAD_DIM, SSD_STATE), 0.1),
        'state_ssd_conv': nrm(4, (L, DEC_BATCH, SSD_CONV - 1, CONV_DIM), 1.0),
        'w_in': nrm(5, (L, D_MODEL, D_IN), D_MODEL ** -0.5),
        'conv_w': nrm(6, (L, SSD_CONV, CONV_DIM), SSD_CONV ** -0.5),
        'conv_b': nrm(7, (L, CONV_DIM), 0.02),
        'dt_bias': dt0 + jnp.log(-jnp.expm1(-dt0)),
        'a_log': jnp.log(jax.random.uniform(ks[8], (L, SSD_HEADS), f32, 1.0, 16.0)),
        'd_skip': 1.0 + nrm(9, (L, SSD_HEADS), 0.1),
        'ssd_norm_g': 1.0 + nrm(10, (L, D_SSD), 0.02),
        'sg_ln_g': 1.0 + nrm(11, (L, D_SG), 0.02),
        'sg_ln_b': nrm(12, (L, D_SG), 0.02),
        'sg_w': nrm(13, (L, SG_GROUPS, SG_CHUNK, SG_CHUNK), SG_CHUNK ** -0.5),
        'sg_b': 1.0 + nrm(14, (L, SG_GROUPS, SG_CHUNK), 0.02),
        'p_ssd': nrm(15, (L, D_SSD, D_MODEL), BETA * D_SSD ** -0.5),
        'p_sg': nrm(16, (L, D_SG, D_MODEL), BETA * D_SG ** -0.5),
        'w_out': nrm(17, (L, D_MODEL, D_MODEL), BETA * D_MODEL ** -0.5),
        'ln1_g': 1.0 + nrm(18, (L, D_MODEL), 0.02),
        'ln1_b': nrm(19, (L, D_MODEL), 0.02),
        'w_router_group': nrm(20, (L, D_MODEL, N_GROUPS), D_MODEL ** -0.5),
        'b_router_group': nrm(21, (L, N_GROUPS), 0.01),
        'w_router_expert': nrm(22, (L, D_MODEL, N_EXPERTS), D_MODEL ** -0.5),
        'b_router_expert': nrm(23, (L, N_EXPERTS), 0.01),
        'w_gate': nrm(24, (L, N_EXPERTS, D_MODEL, D_EXPERT), BETA * D_MODEL ** -0.5),
        'w_up': nrm(25, (L, N_EXPERTS, D_MODEL, D_EXPERT), BETA * D_MODEL ** -0.5),
        'w_down': nrm(26, (L, N_EXPERTS, D_EXPERT, D_MODEL), BETA * D_EXPERT ** -0.5),
        'ln2_g': 1.0 + nrm(27, (L, D_MODEL), 0.02),
        'ln2_b': nrm(28, (L, D_MODEL), 0.02),
    }


def reference(x_prompt, x_sample, state_ssd, state_ssd_conv, w_in, conv_w, conv_b, dt_bias, a_log,
              d_skip, ssd_norm_g, sg_ln_g, sg_ln_b, sg_w, sg_b, p_ssd, p_sg, w_out, ln1_g, ln1_b,
              w_router_group, b_router_group, w_router_expert, b_router_expert, w_gate, w_up, w_down,
              ln2_g, ln2_b):
    st_p, buf_p, st_s, buf_s, v_s = [], [], [], [], []
    for l in range(DEPTH):
        p = dict(w_in=w_in[l], conv_w=conv_w[l], conv_b=conv_b[l], dt_bias=dt_bias[l], a_log=a_log[l],
                 d_skip=d_skip[l], ssd_norm_g=ssd_norm_g[l], sg_ln_g=sg_ln_g[l], sg_ln_b=sg_ln_b[l],
                 sg_w=sg_w[l], sg_b=sg_b[l], p_ssd=p_ssd[l], p_sg=p_sg[l], w_out=w_out[l],
                 ln1_g=ln1_g[l], ln1_b=ln1_b[l], w_router_group=w_router_group[l],
                 b_router_group=b_router_group[l], w_router_expert=w_router_expert[l],
                 b_router_expert=b_router_expert[l], w_gate=w_gate[l], w_up=w_up[l], w_down=w_down[l],
                 ln2_g=ln2_g[l], ln2_b=ln2_b[l])
        bp = x_prompt.shape[0]
        zero_buf = jnp.zeros((bp, SSD_CONV - 1, CONV_DIM), x_prompt.dtype)
        zero_state = jnp.zeros((bp, SSD_HEADS, SSD_HEAD_DIM, SSD_STATE), x_prompt.dtype)
        x_prompt, nb_p, ns_p, _ = decoder_layer(x_prompt, zero_buf, zero_state, p)
        x_sample, nb_s, ns_s, nv_s = decoder_layer(x_sample, state_ssd_conv[l], state_ssd[l], p)
        st_p.append(ns_p)
        buf_p.append(nb_p)
        st_s.append(ns_s)
        buf_s.append(nb_s)
        v_s.append(nv_s)
    ssd_state_prompt = jnp.stack(st_p)
    ssd_conv_prompt = jnp.stack(buf_p)
    ssd_state_sample = jnp.stack(st_s)
    ssd_conv_sample = jnp.stack(buf_s)
    sgu_v_sample = jnp.stack(v_s)
    return (x_prompt, x_sample, ssd_state_prompt, ssd_conv_prompt, ssd_state_sample, ssd_conv_sample, sgu_v_sample)
```

```python
import functools
import math

import jax
import jax.numpy as jnp
from jax import lax
from jax.experimental import pallas as pl
from jax.experimental.pallas import tpu as pltpu

F32 = jnp.float32
BF16 = jnp.bfloat16

D_MODEL = 4096
SSD_HEADS = 64
SSD_HEAD_DIM = 64
D_SSD = SSD_HEADS * SSD_HEAD_DIM
SSD_GROUPS = 8
SSD_STATE = 128
SSD_CONV = 4
D_BC = SSD_GROUPS * SSD_STATE
CONV_DIM = D_SSD + 2 * D_BC
GROUP_W = D_SSD // SSD_GROUPS
SG_GROUPS = 8
D_SG = 2048
SG_HEAD_DIM = D_SG // SG_GROUPS
SG_CHUNK = 128
N_GROUPS = 8
EXPERTS_PER_GROUP = 4
N_EXPERTS = N_GROUPS * EXPERTS_PER_GROUP
D_EXPERT = 512
DEPTH = 1
ALPHA = (2 * DEPTH) ** 0.25
LN_EPS = 1e-5
RMS_EPS = 1e-5

LANE = 128
SUB = 8
CHUNK = 128
HALO = 8
VMEM_LIMIT = 58 * 1024 * 1024
NEG = -1e30
META_W = LANE
MOE_TM = 256


def _cp(sem, **kw):
    return pltpu.CompilerParams(dimension_semantics=sem, vmem_limit_bytes=VMEM_LIMIT, **kw)


def _tile(n, pref, align):
    if n <= pref:
        return n
    t = (pref // align) * align
    while t > align and n % t:
        t -= align
    assert n % t == 0, (n, pref, align)
    return t


def _silu(a):
    return a * jax.nn.sigmoid(a)


def _softplus(a):
    return jnp.maximum(a, 0.0) + jnp.log1p(jnp.exp(-jnp.abs(a)))


def _split3(v):
    hi = v.astype(BF16)
    r1 = v - hi.astype(F32)
    mid = r1.astype(BF16)
    lo = (r1 - mid.astype(F32)).astype(BF16)
    return hi, mid, lo


def _dot(a, b):
    return jnp.dot(a, b, preferred_element_type=F32)


def _dot_nt(a, b):
    return lax.dot_general(a, b, (((1,), (1,)), ((), ())), preferred_element_type=F32)


def _spread(v, m):
    hi, mid, lo = _split3(v)
    return _dot(hi, m) + _dot(mid, m) + _dot(lo, m)


def _cumsum_rows(v):
    n = v.shape[0]
    row = lax.broadcasted_iota(jnp.int32, v.shape, 0)
    s = 1
    while s < n:
        v = v + jnp.where(row >= s, pltpu.roll(v, s, axis=0), 0.0)
        s *= 2
    return v


def _proj_kernel(x_ref, w_ref, o_ref, *, act):
    o_ref[...] = act(_dot(x_ref[...], w_ref[...])).astype(o_ref.dtype)


def _proj(x, w, col0, ncols, act, out_dtype, name):
    t, k = x.shape
    tm = _tile(t, 1024, 128)
    tn = _tile(ncols, 1024, 128)
    assert col0 % tn == 0
    return pl.pallas_call(
        functools.partial(_proj_kernel, act=act),
        out_shape=jax.ShapeDtypeStruct((t, ncols), out_dtype),
        grid=(ncols // tn, t // tm),
        in_specs=[pl.BlockSpec((tm, k), lambda j, i: (i, 0)),
                  pl.BlockSpec((k, tn), lambda j, i: (0, j + col0 // tn))],
        out_specs=pl.BlockSpec((tm, tn), lambda j, i: (i, j)),
        compiler_params=_cp(("arbitrary", "arbitrary")),
        name=name)(x, w)


def _dt_kernel(x_ref, w_ref, b_ref, o_ref):
    o_ref[...] = _softplus(_dot(x_ref[...], w_ref[...]) + b_ref[...])


def _dt_proj(x, w_dt, dt_bias):
    t, k = x.shape
    tm = _tile(t, 1024, 128)
    return pl.pallas_call(
        _dt_kernel,
        out_shape=jax.ShapeDtypeStruct((t, LANE), F32),
        grid=(t // tm,),
        in_specs=[pl.BlockSpec((tm, k), lambda i: (i, 0)),
                  pl.BlockSpec((k, LANE), lambda i: (0, 0)),
                  pl.BlockSpec((1, LANE), lambda i: (0, 0))],
        out_specs=pl.BlockSpec((tm, LANE), lambda i: (i, 0)),
        compiler_params=_cp(("arbitrary",)),
        name="proj_dt")(x, w_dt, dt_bias)


def _conv_silu(xpad_ref, cw_ref, cb_ref, lo, width, rows):
    acc = cb_ref[:, lo:lo + width]
    for k in range(SSD_CONV):
        start = HALO - (SSD_CONV - 1) + k
        acc = acc + cw_ref[k:k + 1, lo:lo + width] * xpad_ref[pl.ds(start, rows), lo:lo + width]
    return _silu(acc)


def _gated_rms(y, zs, g):
    yz = y * zs
    ms = jnp.mean(yz * yz, axis=-1, keepdims=True)
    return yz * lax.rsqrt(ms + RMS_EPS) * g


def _ssd_prompt_kernel(xbc_ref, dt_ref, zs_ref, cw_ref, cb_ref, alog_ref, dsk_ref, ng_ref, e_ref,
                       y_ref, st_ref, tail_ref, xpad_ref, stt_ref):
    c = pl.program_id(1)
    q = CHUNK

    @pl.when(c == 0)
    def _():
        xpad_ref[0:HALO, :] = jnp.zeros((HALO, CONV_DIM), F32)
        stt_ref[...] = jnp.zeros_like(stt_ref)

    xpad_ref[HALO:HALO + q, :] = xbc_ref[...]

    dtv = dt_ref[...]
    a = -jnp.exp(alog_ref[...])
    acs = _cumsum_rows(dtv * a)
    last = acs[q - 1:q, :]
    te = jnp.exp(last - acs)
    eacs = jnp.exp(acs)
    cd = jnp.broadcast_to(jnp.exp(last), (SUB, LANE))
    dsk = jnp.broadcast_to(dsk_ref[...], (SUB, LANE))
    stack = jnp.concatenate([dtv, te, eacs, cd, dsk], axis=0)
    acs_t = acs.T

    ii = lax.broadcasted_iota(jnp.int32, (q, q), 0)
    jj = lax.broadcasted_iota(jnp.int32, (q, q), 1)
    causal = ii >= jj
    lane = lax.broadcasted_iota(jnp.int32, (q, LANE), 1)
    first_half = lane < SSD_HEAD_DIM

    for g in range(SSD_GROUPS):
        lo = g * GROUP_W
        xs = _conv_silu(xpad_ref, cw_ref, cb_ref, lo, GROUP_W, q)
        bm = _conv_silu(xpad_ref, cw_ref, cb_ref, D_SSD + g * SSD_STATE, SSD_STATE, q)
        cm = _conv_silu(xpad_ref, cw_ref, cb_ref, D_SSD + D_BC + g * SSD_STATE, SSD_STATE, q)
        ex = _spread(stack, e_ref[:, lo:lo + GROUP_W])
        dt_e, te_e, eacs_e = ex[0:q], ex[q:2 * q], ex[2 * q:3 * q]
        cd_e = ex[3 * q:3 * q + 1]
        dsk_e = ex[3 * q + SUB:3 * q + SUB + 1]
        xdt = xs * dt_e
        xdt_b = xdt.astype(BF16)
        xw_b = (xdt * te_e).astype(BF16)
        bm_b = bm.astype(BF16)
        cm_b = cm.astype(BF16)
        scores = _dot_nt(cm_b, bm_b)
        st_g = stt_ref[:, lo:lo + GROUP_W]
        y_off = _dot(cm_b, st_g.astype(BF16)) * eacs_e
        parts = []
        for k in range(GROUP_W // LANE):
            ms = []
            for hh in range(2):
                h = g * (SSD_HEADS // SSD_GROUPS) + 2 * k + hh
                seg = acs[:, h:h + 1] - acs_t[h:h + 1, :]
                ms.append((scores * jnp.exp(jnp.where(causal, seg, NEG))).astype(BF16))
            xp = xdt_b[:, k * LANE:(k + 1) * LANE]
            zero = jnp.zeros_like(xp)
            rhs = jnp.concatenate([jnp.where(first_half, xp, zero), jnp.where(first_half, zero, xp)], axis=0)
            parts.append(_dot(jnp.concatenate(ms, axis=1), rhs))
        y = jnp.concatenate(parts, axis=1) + y_off + dsk_e * xs
        zs = zs_ref[:, lo:lo + GROUP_W].astype(F32)
        y_ref[:, lo:lo + GROUP_W] = _gated_rms(y, zs, ng_ref[:, lo:lo + GROUP_W]).astype(y_ref.dtype)
        stt_ref[:, lo:lo + GROUP_W] = st_g * cd_e + _dot(bm.T.astype(BF16), xw_b)

    xpad_ref[0:HALO, :] = xpad_ref[q:q + HALO, :]

    @pl.when(c == pl.num_programs(1) - 1)
    def _():
        tail_ref[0] = xpad_ref[HALO + q - (SSD_CONV - 1):HALO + q, :]
        for g in range(SSD_GROUPS):
            lo = g * GROUP_W
            st_ref[0, lo:lo + GROUP_W, :] = stt_ref[:, lo:lo + GROUP_W].T


def _ssd_prompt(xbc, dt, zs, cw, cb, alog, dsk, ng, e_mat, nb, nc):
    t = nb * nc * CHUNK
    q = CHUNK
    row = lambda b, c: (b * nc + c, 0)
    const = lambda b, c: (0, 0)
    return pl.pallas_call(
        _ssd_prompt_kernel,
        out_shape=(jax.ShapeDtypeStruct((t, D_SSD), BF16),
                   jax.ShapeDtypeStruct((nb, D_SSD, SSD_STATE), F32),
                   jax.ShapeDtypeStruct((nb, SSD_CONV - 1, CONV_DIM), F32)),
        grid=(nb, nc),
        in_specs=[pl.BlockSpec((q, CONV_DIM), row),
                  pl.BlockSpec((q, LANE), row),
                  pl.BlockSpec((q, D_SSD), row),
                  pl.BlockSpec((SSD_CONV, CONV_DIM), const),
                  pl.BlockSpec((1, CONV_DIM), const),
                  pl.BlockSpec((1, LANE), const),
                  pl.BlockSpec((1, LANE), const),
                  pl.BlockSpec((1, D_SSD), const),
                  pl.BlockSpec((LANE, D_SSD), const)],
        out_specs=(pl.BlockSpec((q, D_SSD), row),
                   pl.BlockSpec((1, D_SSD, SSD_STATE), lambda b, c: (b, 0, 0)),
                   pl.BlockSpec((1, SSD_CONV - 1, CONV_DIM), lambda b, c: (b, 0, 0))),
        scratch_shapes=[pltpu.VMEM((HALO + q, CONV_DIM), F32),
                        pltpu.VMEM((SSD_STATE, D_SSD), F32)],
        compiler_params=_cp(("arbitrary", "arbitrary")),
        name="ssd_prompt")(xbc, dt, zs, cw, cb, alog, dsk, ng, e_mat)


def _ssd_sample_kernel(xbc_ref, dt_ref, zs_ref, buf_ref, s0_ref, cw_ref, cb_ref, alog_ref, dsk_ref,
                       ng_ref, e_ref, et_ref, g2_ref, y_ref, st_ref, tail_ref, xpad_ref,
                       *, nseq, n_real):
    r = SUB
    a = -jnp.exp(alog_ref[...])
    row = lax.broadcasted_iota(jnp.int32, (r, LANE), 0)
    real = row < n_real
    dsk = jnp.broadcast_to(dsk_ref[...], (r, LANE))

    per_seq = []
    stacks = []
    for s in range(nseq):
        xpad_ref[s, HALO - (SSD_CONV - 1):HALO, :] = buf_ref[s]
        xpad_ref[s, HALO:HALO + r, :] = xbc_ref[s * r:(s + 1) * r, :]
        tail_ref[s] = xpad_ref[s, HALO + n_real - (SSD_CONV - 1):HALO + n_real, :]
        xp_s = xpad_ref.at[s]
        xs = _conv_silu(xp_s, cw_ref, cb_ref, 0, D_SSD, r)
        bm = _conv_silu(xp_s, cw_ref, cb_ref, D_SSD, D_BC, r)
        cm = _conv_silu(xp_s, cw_ref, cb_ref, D_SSD + D_BC, D_BC, r)
        dtv = jnp.where(real, dt_ref[s * r:(s + 1) * r, :], 0.0)
        acs = _cumsum_rows(dtv * a)
        last = acs[r - 1:r, :]
        te = jnp.exp(last - acs)
        eacs = jnp.exp(acs)
        cd = jnp.exp(last)
        coefs = []
        for j in range(n_real):
            sc = _spread(cm * bm[j:j + 1, :], g2_ref[...])
            dec = jnp.where(row >= j, jnp.exp(acs - acs[j:j + 1, :]), 0.0)
            coefs.append(sc * dec)
        stacks.append(jnp.concatenate([dtv, te, eacs, dsk] + coefs, axis=0))
        per_seq.append((xs, bm, cm, cd))

    ex_all = _spread(jnp.concatenate(stacks, axis=0), e_ref[...])
    blk = (4 + n_real) * r
    for s in range(nseq):
        xs, bm, cm, cd = per_seq[s]
        ex = ex_all[s * blk:(s + 1) * blk]
        dt_e, te_e, eacs_e, dsk_e = ex[0:r], ex[r:2 * r], ex[2 * r:3 * r], ex[3 * r:4 * r]
        xdt = xs * dt_e
        xw = xdt * te_e
        y = dsk_e * xs
        for j in range(n_real):
            y = y + ex[(4 + j) * r:(5 + j) * r] * xdt[j:j + 1, :]
        cm_b = cm.astype(BF16)
        cdmat = _spread_cols(cd, et_ref)
        offs = []
        for g in range(SSD_GROUPS):
            lo = g * GROUP_W
            s_g = s0_ref[s, lo:lo + GROUP_W, :]
            offs.append(_dot_nt(cm_b[:, g * SSD_STATE:(g + 1) * SSD_STATE], s_g.astype(BF16)))
            upd = _dot(xw[:, lo:lo + GROUP_W].T, bm[:, g * SSD_STATE:(g + 1) * SSD_STATE])
            st_ref[s, lo:lo + GROUP_W, :] = s_g * cdmat[lo:lo + GROUP_W, :] + upd
        y = y + jnp.concatenate(offs, axis=1) * eacs_e
        zs = zs_ref[s * r:(s + 1) * r, :].astype(F32)
        outs = []
        for g in range(SSD_GROUPS):
            lo = g * GROUP_W
            outs.append(_gated_rms(y[:, lo:lo + GROUP_W], zs[:, lo:lo + GROUP_W], ng_ref[:, lo:lo + GROUP_W]))
        y_ref[s * r:(s + 1) * r, :] = jnp.concatenate(outs, axis=1).astype(y_ref.dtype)


def _spread_cols(v, et_ref):
    rows = jnp.broadcast_to(v, (SSD_STATE, LANE))
    hi, mid, lo = _split3(rows)
    et = et_ref[...]
    return _dot_nt(et, hi) + _dot_nt(et, mid) + _dot_nt(et, lo)


def _ssd_sample(xbc, dt, zs, buf, s0, cw, cb, alog, dsk, ng, e_mat, et_mat, g2_mat, row0, ns, n_real):
    nseq = 2
    assert ns % nseq == 0 and row0 % (nseq * SUB) == 0
    rb = row0 // (nseq * SUB)
    row = lambda i: (rb + i, 0)
    const = lambda i: (0, 0)
    seq3 = lambda i: (i, 0, 0)
    return pl.pallas_call(
        functools.partial(_ssd_sample_kernel, nseq=nseq, n_real=n_real),
        out_shape=(jax.ShapeDtypeStruct((ns * SUB, D_SSD), BF16),
                   jax.ShapeDtypeStruct((ns, D_SSD, SSD_STATE), F32),
                   jax.ShapeDtypeStruct((ns, SSD_CONV - 1, CONV_DIM), F32)),
        grid=(ns // nseq,),
        in_specs=[pl.BlockSpec((nseq * SUB, CONV_DIM), row),
                  pl.BlockSpec((nseq * SUB, LANE), row),
                  pl.BlockSpec((nseq * SUB, D_SSD), row),
                  pl.BlockSpec((nseq, SSD_CONV - 1, CONV_DIM), seq3),
                  pl.BlockSpec((nseq, D_SSD, SSD_STATE), seq3),
                  pl.BlockSpec((SSD_CONV, CONV_DIM), const),
                  pl.BlockSpec((1, CONV_DIM), const),
                  pl.BlockSpec((1, LANE), const),
                  pl.BlockSpec((1, LANE), const),
                  pl.BlockSpec((1, D_SSD), const),
                  pl.BlockSpec((LANE, D_SSD), const),
                  pl.BlockSpec((D_SSD, LANE), const),
                  pl.BlockSpec((D_BC, LANE), const)],
        out_specs=(pl.BlockSpec((nseq * SUB, D_SSD), lambda i: (i, 0)),
                   pl.BlockSpec((nseq, D_SSD, SSD_STATE), seq3),
                   pl.BlockSpec((nseq, SSD_CONV - 1, CONV_DIM), seq3)),
        scratch_shapes=[pltpu.VMEM((nseq, HALO + SUB, CONV_DIM), F32)],
        compiler_params=_cp(("arbitrary",)),
        name="ssd_sample")(xbc, dt, zs, buf, s0, cw, cb, alog, dsk, ng, e_mat, et_mat, g2_mat)


def _sgu_kernel(u_ref, gv_ref, lg_ref, lb_ref, w_ref, bt_ref, y_ref, *v_out, period):
    gv = gv_ref[...]
    mu = jnp.mean(gv, axis=-1, keepdims=True)
    d = gv - mu
    var = jnp.mean(d * d, axis=-1, keepdims=True)
    v = d * lax.rsqrt(var + LN_EPS) * lg_ref[...] + lb_ref[...]
    if v_out:
        v_out[0][...] = v
    v_b = v.astype(BF16)
    n = SG_CHUNK
    ii = lax.broadcasted_iota(jnp.int32, (n, n), 0)
    jj = lax.broadcasted_iota(jnp.int32, (n, n), 1)
    keep = (jj <= ii) & ((ii // period) == (jj // period))
    for g in range(SG_GROUPS):
        lo = g * SG_HEAD_DIM
        w = jnp.where(keep, w_ref[g], 0.0).astype(BF16)
        mixed = _dot(w, v_b[:, lo:lo + SG_HEAD_DIM]) + bt_ref[:, g:g + 1]
        y_ref[:, lo:lo + SG_HEAD_DIM] = (u_ref[:, lo:lo + SG_HEAD_DIM].astype(F32) * mixed).astype(y_ref.dtype)


def _sgu(u, gv, lg, lb, w, bt, row0, nrows, period, want_v):
    n = SG_CHUNK
    rb = row0 // n
    row = lambda i: (rb + i, 0)
    own = lambda i: (i, 0)
    const = lambda i: (0, 0)
    out_shape = [jax.ShapeDtypeStruct((nrows, D_SG), BF16)]
    out_specs = [pl.BlockSpec((n, D_SG), own)]
    if want_v:
        out_shape.append(jax.ShapeDtypeStruct((nrows, D_SG), F32))
        out_specs.append(pl.BlockSpec((n, D_SG), own))
    return pl.pallas_call(
        functools.partial(_sgu_kernel, period=period),
        out_shape=tuple(out_shape),
        grid=(nrows // n,),
        in_specs=[pl.BlockSpec((n, D_SG), row),
                  pl.BlockSpec((n, D_SG), row),
                  pl.BlockSpec((1, D_SG), const),
                  pl.BlockSpec((1, D_SG), const),
                  pl.BlockSpec((SG_GROUPS, n, n), lambda i: (0, 0, 0)),
                  pl.BlockSpec((n, SG_GROUPS), const)],
        out_specs=tuple(out_specs),
        compiler_params=_cp(("arbitrary",)),
        name="sgu_sample" if want_v else "sgu_prompt")(u, gv, lg, lb, w, bt)


def _merge_kernel(yap_ref, ybp_ref, yas_ref, ybs_ref, pa_ref, pb_ref, ga_ref, gb_ref, o_ref, *, n_prompt_tiles):
    i = pl.program_id(1)

    def body(ya_ref, yb_ref):
        a = _dot(ya_ref[...], pa_ref[...])
        b = _dot(yb_ref[...], pb_ref[...])
        o_ref[...] = (ga_ref[...].astype(F32) * a + gb_ref[...].astype(F32) * b).astype(o_ref.dtype)

    @pl.when(i < n_prompt_tiles)
    def _():
        body(yap_ref, ybp_ref)

    @pl.when(i >= n_prompt_tiles)
    def _():
        body(yas_ref, ybs_ref)


def _merge(ya_p, yb_p, ya_s, yb_s, p_ssd, p_sg, gates):
    tp, tsp = ya_p.shape[0], ya_s.shape[0]
    tm = _tile(math.gcd(tp, tsp), 256, 128)
    npt = tp // tm
    tn = 1024
    nb = D_MODEL // tn
    prow = lambda j, i: (jnp.minimum(i, npt - 1), 0)
    srow = lambda j, i: (jnp.maximum(i - npt, 0), 0)
    return pl.pallas_call(
        functools.partial(_merge_kernel, n_prompt_tiles=npt),
        out_shape=jax.ShapeDtypeStruct((tp + tsp, D_MODEL), BF16),
        grid=(nb, (tp + tsp) // tm),
        in_specs=[pl.BlockSpec((tm, D_SSD), prow),
                  pl.BlockSpec((tm, D_SG), prow),
                  pl.BlockSpec((tm, D_SSD), srow),
                  pl.BlockSpec((tm, D_SG), srow),
                  pl.BlockSpec((D_SSD, tn), lambda j, i: (0, j)),
                  pl.BlockSpec((D_SG, tn), lambda j, i: (0, j)),
                  pl.BlockSpec((tm, tn), lambda j, i: (i, j)),
                  pl.BlockSpec((tm, tn), lambda j, i: (i, j + nb))],
        out_specs=pl.BlockSpec((tm, tn), lambda j, i: (i, j)),
        compiler_params=_cp(("arbitrary", "arbitrary")),
        name="merge")(ya_p, yb_p, ya_s, yb_s, p_ssd, p_sg, gates, gates)


def _layer_norm(v, g, b):
    mu = jnp.mean(v, axis=-1, keepdims=True)
    d = v - mu
    var = jnp.mean(d * d, axis=-1, keepdims=True)
    return d * lax.rsqrt(var + LN_EPS) * g + b


def _route(logits, valid, carry_ref):
    tm = logits.shape[0]
    lane = lax.broadcasted_iota(jnp.int32, (tm, LANE), 1).astype(F32)
    big = float(LANE)
    is_g = lane < N_GROUPS
    gl = jnp.where(is_g, logits, -jnp.inf)
    gmax = jnp.max(gl, axis=-1, keepdims=True)
    grp = jnp.min(jnp.where(gl == gmax, lane, big), axis=-1, keepdims=True)
    p_grp = 1.0 / jnp.sum(jnp.where(is_g, jnp.exp(gl - gmax), 0.0), axis=-1, keepdims=True)
    e0 = N_GROUPS + grp * EXPERTS_PER_GROUP
    in_grp = (lane >= e0) & (lane < e0 + EXPERTS_PER_GROUP)
    el = jnp.where(in_grp, logits, -jnp.inf)
    t1 = jnp.max(el, axis=-1, keepdims=True)
    i1 = jnp.min(jnp.where(el == t1, lane, big), axis=-1, keepdims=True)
    el2 = jnp.where(lane == i1, -jnp.inf, el)
    t2 = jnp.max(el2, axis=-1, keepdims=True)
    i2 = jnp.min(jnp.where(el2 == t2, lane, big), axis=-1, keepdims=True)
    ex = jnp.exp(t2 - t1)
    w1 = p_grp / (1.0 + ex)
    w2 = p_grp * ex / (1.0 + ex)
    onehot = jnp.where((lane == grp) & valid, 1.0, 0.0)
    ri = lax.broadcasted_iota(jnp.int32, (tm, tm), 0)
    ci = lax.broadcasted_iota(jnp.int32, (tm, tm), 1)
    before = jnp.where(ci < ri, 1.0, 0.0).astype(BF16)
    prefix = _dot(before, onehot.astype(BF16)) + carry_ref[...]
    rank = jnp.sum(prefix * onehot, axis=-1, keepdims=True)
    carry_ref[...] = carry_ref[...] + jnp.sum(onehot, axis=0, keepdims=True)
    meta = jnp.where(lane == 0, jnp.where(valid, grp, -1.0), 0.0)
    meta = jnp.where(lane == 1, rank, meta)
    for k in range(EXPERTS_PER_GROUP):
        gk = jnp.where(i1 == e0 + k, w1, jnp.where(i2 == e0 + k, w2, 0.0))
        meta = jnp.where(lane == 2 + k, gk, meta)
    return meta


def _outproj_kernel(m_ref, w_ref, xp_ref, xs_ref, g_ref, b_ref, wr_ref, br_ref, o_ref, cnt_ref,
                    *, n_prompt_tiles, tn, n_real):
    i = pl.program_id(0)
    j = pl.program_id(1)
    nj = pl.num_programs(1)

    @pl.when((i == 0) & (j == 0))
    def _():
        cnt_ref[...] = jnp.zeros_like(cnt_ref)

    out = _dot(m_ref[...], w_ref[...])
    x = jnp.where(i < n_prompt_tiles, xp_ref[...], xs_ref[...])
    for jv in range(D_MODEL // tn):
        @pl.when(j == jv)
        def _():
            o_ref[:, jv * tn:(jv + 1) * tn] = ALPHA * x + out

    @pl.when(j == nj - 1)
    def _():
        x1 = _layer_norm(o_ref[:, 0:D_MODEL], g_ref[...], b_ref[...])
        o_ref[:, 0:D_MODEL] = x1
        logits = jnp.dot(x1, wr_ref[...], preferred_element_type=F32,
                         precision=lax.Precision.HIGHEST) + br_ref[...]
        tm = x1.shape[0]
        r = lax.broadcasted_iota(jnp.int32, (tm, LANE), 0)
        valid = (i < n_prompt_tiles) | ((r % SUB) < n_real)
        o_ref[:, D_MODEL:D_MODEL + META_W] = _route(logits, valid, cnt_ref)


def _outproj(merged, w_out, xp, xs, g, b, wr, br, n_real):
    t = merged.shape[0]
    tp = xp.shape[0]
    tm = _tile(math.gcd(tp, xs.shape[0]), 256, 128)
    assert t % tm == 0 and xs.shape[0] % tm == 0
    tn = 1024
    npt = tp // tm
    last_j = D_MODEL // tn - 1
    const = lambda i, j: (0, 0)
    return pl.pallas_call(
        functools.partial(_outproj_kernel, n_prompt_tiles=npt, tn=tn, n_real=n_real),
        out_shape=(jax.ShapeDtypeStruct((t, D_MODEL + META_W), F32),
                   jax.ShapeDtypeStruct((1, LANE), F32)),
        grid=(t // tm, D_MODEL // tn),
        in_specs=[pl.BlockSpec((tm, D_MODEL), lambda i, j: (i, 0)),
                  pl.BlockSpec((D_MODEL, tn), lambda i, j: (0, j)),
                  pl.BlockSpec((tm, tn), lambda i, j: (jnp.minimum(i, npt - 1), jnp.where(i < npt, j, last_j))),
                  pl.BlockSpec((tm, tn), lambda i, j: (jnp.maximum(i - npt, 0), jnp.where(i < npt, 0, j))),
                  pl.BlockSpec((1, D_MODEL), const),
                  pl.BlockSpec((1, D_MODEL), const),
                  pl.BlockSpec((D_MODEL, LANE), const),
                  pl.BlockSpec((1, LANE), const)],
        out_specs=(pl.BlockSpec((tm, D_MODEL + META_W), lambda i, j: (i, 0)),
                   pl.BlockSpec((1, LANE), const)),
        compiler_params=_cp(("arbitrary", "arbitrary")),
        name="outproj_ln_route")(merged, w_out, xp, xs, g, b, wr, br)


def _moe_kernel(src_ref, dst_ref, tgrp_ref, nused_ref, x_hbm, wg_ref, wu_ref, wd_ref, g_ref, b_ref,
                yp_hbm, ys_hbm, xg_ref, xb_ref, acc_ref, ob_ref, gsem, ssem, *, tm, n_prompt):
    t = pl.program_id(0)
    e = pl.program_id(1)
    n_used = nused_ref[0]
    slot = t % 2

    def gather(tile, sl, start):
        def body(r, carry):
            s = jnp.maximum(src_ref[tile * tm + r], 0)
            cp = pltpu.make_async_copy(x_hbm.at[pl.ds(s, 1)], xg_ref.at[sl, pl.ds(r, 1)], gsem.at[sl])
            if start:
                cp.start()
            else:
                cp.wait()
            return carry
        lax.fori_loop(0, tm, body, 0)

    def scatter(tile, start):
        def body(r, carry):
            d = dst_ref[tile * tm + r]

            @pl.when((d >= 0) & (d < n_prompt))
            def _():
                cp = pltpu.make_async_copy(ob_ref.at[pl.ds(r, 1)], yp_hbm.at[pl.ds(d, 1)], ssem.at[0])
                cp.start() if start else cp.wait()

            @pl.when(d >= n_prompt)
            def _():
                cp = pltpu.make_async_copy(ob_ref.at[pl.ds(r, 1)], ys_hbm.at[pl.ds(d - n_prompt, 1)], ssem.at[0])
                cp.start() if start else cp.wait()
            return carry
        lax.fori_loop(0, tm, body, 0)

    @pl.when(t < n_used)
    def _():
        @pl.when(e == 0)
        def _():
            @pl.when(t == 0)
            def _():
                gather(0, 0, True)
            gather(t, slot, False)

            @pl.when(t + 1 < n_used)
            def _():
                gather(t + 1, 1 - slot, True)
            xb_ref[...] = xg_ref[slot, :, 0:D_MODEL].astype(BF16)

        xb = xb_ref[...]
        lane = lax.broadcasted_iota(jnp.int32, (tm, META_W), 1)
        gate = jnp.sum(jnp.where(lane == 2 + e, xg_ref[slot, :, D_MODEL:D_MODEL + META_W], 0.0),
                       axis=-1, keepdims=True)
        hg = _dot(xb, wg_ref[0])
        hu = _dot(xb, wu_ref[0])
        hh = (_silu(hg) * hu * gate).astype(BF16)
        part = _dot(hh, wd_ref[0])

        @pl.when(e == 0)
        def _():
            acc_ref[...] = part

        @pl.when(e > 0)
        def _():
            acc_ref[...] = acc_ref[...] + part

        @pl.when(e == EXPERTS_PER_GROUP - 1)
        def _():
            @pl.when(t > 0)
            def _():
                scatter(t - 1, False)
            ob_ref[...] = _layer_norm(ALPHA * xg_ref[slot, :, 0:D_MODEL] + acc_ref[...], g_ref[...], b_ref[...])
            scatter(t, True)

            @pl.when(t == n_used - 1)
            def _():
                scatter(t, False)


def _moe(x1e, src, dst, tile_group, n_used, wg, wu, wd, g, b, n_prompt, n_sample, max_tiles):
    tm = MOE_TM
    def wmap(t, e, src, dst, tg, nu):
        last = tg[nu[0] - 1] * EXPERTS_PER_GROUP + EXPERTS_PER_GROUP - 1
        return (jnp.where(t < nu[0], tg[t] * EXPERTS_PER_GROUP + e, last), 0, 0)
    const = lambda t, e, src, dst, tg, nu: (0, 0)
    grid_spec = pltpu.PrefetchScalarGridSpec(
        num_scalar_prefetch=4,
        grid=(max_tiles, EXPERTS_PER_GROUP),
        in_specs=[pl.BlockSpec(memory_space=pl.ANY),
                  pl.BlockSpec((1, D_MODEL, D_EXPERT), wmap),
                  pl.BlockSpec((1, D_MODEL, D_EXPERT), wmap),
                  pl.BlockSpec((1, D_EXPERT, D_MODEL), wmap),
                  pl.BlockSpec((1, D_MODEL), const),
                  pl.BlockSpec((1, D_MODEL), const)],
        out_specs=(pl.BlockSpec(memory_space=pl.ANY), pl.BlockSpec(memory_space=pl.ANY)),
        scratch_shapes=[pltpu.VMEM((2, tm, D_MODEL + META_W), F32),
                        pltpu.VMEM((tm, D_MODEL), BF16),
                        pltpu.VMEM((tm, D_MODEL), F32),
                        pltpu.VMEM((tm, D_MODEL), F32),
                        pltpu.SemaphoreType.DMA((2,)),
                        pltpu.SemaphoreType.DMA((1,))])
    return pl.pallas_call(
        functools.partial(_moe_kernel, tm=tm, n_prompt=n_prompt),
        out_shape=(jax.ShapeDtypeStruct((n_prompt, D_MODEL), F32),
                   jax.ShapeDtypeStruct((n_sample, D_MODEL), F32)),
        grid_spec=grid_spec,
        compiler_params=_cp(("arbitrary", "arbitrary"), has_side_effects=True),
        name="moe_ln2")(src, dst, tile_group, n_used, x1e, wg, wu, wd, g, b)


def _dispatch(meta, counts, n_prompt, ns, n_real, max_tiles):
    tm = MOE_TM
    t = meta.shape[0]
    grp = meta[:, 0].astype(jnp.int32)
    rank = meta[:, 1].astype(jnp.int32)
    cnt = counts[0, :N_GROUPS].astype(jnp.int32)
    tiles = (cnt + tm - 1) // tm
    tile_end = jnp.cumsum(tiles)
    row_start = (tile_end - tiles) * tm
    valid = grp >= 0
    pos = jnp.where(valid, row_start[jnp.maximum(grp, 0)] + rank, max_tiles * tm)
    rows = jnp.arange(t, dtype=jnp.int32)
    src = jnp.full((max_tiles * tm,), -1, jnp.int32).at[pos].set(rows, mode="drop")
    rel = src - n_prompt
    dst = jnp.where(src < n_prompt, src, n_prompt + (rel // SUB) * n_real + rel % SUB)
    dst = jnp.where(src < 0, -1, dst)
    tile_group = jnp.minimum(jnp.searchsorted(tile_end, jnp.arange(max_tiles, dtype=jnp.int32), side="right"),
                             N_GROUPS - 1).astype(jnp.int32)
    return src, dst, tile_group, tile_end[-1:].astype(jnp.int32)


def _head_spread_matrices():
    h = lax.broadcasted_iota(jnp.int32, (LANE, D_SSD), 0)
    c = lax.broadcasted_iota(jnp.int32, (LANE, D_SSD), 1)
    e_mat = (c // SSD_HEAD_DIM == h).astype(BF16)
    n = lax.broadcasted_iota(jnp.int32, (D_BC, LANE), 0)
    hh = lax.broadcasted_iota(jnp.int32, (D_BC, LANE), 1)
    g2 = ((n // SSD_STATE == hh // (SSD_HEADS // SSD_GROUPS)) & (hh < SSD_HEADS)).astype(BF16)
    return e_mat, e_mat.T, g2


def _pad_lanes(v, width=LANE):
    return jnp.pad(v, ((0, 0), (0, width - v.shape[-1])))


def kernel(x_prompt, x_sample, state_ssd, state_ssd_conv, w_in, conv_w, conv_b, dt_bias, a_log, d_skip, ssd_norm_g, sg_ln_g, sg_ln_b, sg_w, sg_b, p_ssd, p_sg, w_out, ln1_g, ln1_b, w_router_group, b_router_group, w_router_expert, b_router_expert, w_gate, w_up, w_down, ln2_g, ln2_b):
    nb, seq, _ = x_prompt.shape
    ns, n_real, _ = x_sample.shape
    assert seq % CHUNK == 0 and n_real <= SUB and SSD_CONV - 1 <= n_real
    tp = nb * seq
    tsp = ns * SUB
    ts = ns * n_real
    assert tsp % CHUNK == 0
    nc = seq // CHUNK

    xp = x_prompt.reshape(tp, D_MODEL)
    xs_pad = jnp.pad(x_sample, ((0, 0), (0, SUB - n_real), (0, 0))).reshape(tsp, D_MODEL)
    x_all = jnp.concatenate([xp.astype(BF16), xs_pad.astype(BF16)], axis=0)

    w = w_in[0]
    c_dt = D_SSD + CONV_DIM
    w_head = w[:, :c_dt].astype(BF16)
    w_dt = _pad_lanes(w[:, c_dt:c_dt + SSD_HEADS]).astype(BF16)
    w_tail = w[:, c_dt + SSD_HEADS:].astype(BF16)

    zs = _proj(x_all, w_head, 0, D_SSD, _silu, BF16, "proj_z")
    xbc = _proj(x_all, w_head, D_SSD, CONV_DIM, lambda a: a, F32, "proj_xbc")
    dt = _dt_proj(x_all, w_dt, _pad_lanes(dt_bias))
    u = _proj(x_all, w_tail, 0, D_SG, jax.nn.gelu, BF16, "proj_u")
    gv = _proj(x_all, w_tail, D_SG, D_SG, jax.nn.gelu, F32, "proj_v")
    gates = _proj(x_all, w_tail, 2 * D_SG, 2 * D_MODEL, jax.nn.sigmoid, BF16, "proj_gates")

    e_mat, et_mat, g2_mat = _head_spread_matrices()
    alog = _pad_lanes(a_log)
    dsk = _pad_lanes(d_skip)
    ya_p, st_p, tail_p = _ssd_prompt(xbc, dt, zs, conv_w[0], conv_b, alog, dsk, ssd_norm_g, e_mat, nb, nc)
    ya_s, st_s, tail_s = _ssd_sample(xbc, dt, zs, state_ssd_conv[0],
                                     state_ssd[0].reshape(ns, D_SSD, SSD_STATE), conv_w[0], conv_b,
                                     alog, dsk, ssd_norm_g, e_mat, et_mat, g2_mat, tp, ns, n_real)

    bt_p = sg_b[0].T
    (yb_p,) = _sgu(u, gv, sg_ln_g, sg_ln_b, sg_w[0], bt_p, 0, tp, SG_CHUNK, False)
    reps = SG_CHUNK // SUB
    w_s = jnp.tile(sg_w[0][:, :SUB, :SUB], (1, reps, reps))
    bt_s = jnp.tile(sg_b[0][:, :SUB].T, (reps, 1))
    yb_s, v_s = _sgu(u, gv, sg_ln_g, sg_ln_b, w_s, bt_s, tp, tsp, SUB, True)

    merged = _merge(ya_p, yb_p, ya_s, yb_s, p_ssd[0].astype(BF16), p_sg[0].astype(BF16), gates)

    wr = _pad_lanes(jnp.concatenate([w_router_group[0], w_router_expert[0]], axis=1))
    br = _pad_lanes(jnp.concatenate([b_router_group, b_router_expert], axis=1))
    x1e, counts = _outproj(merged, w_out[0].astype(BF16), xp, xs_pad, ln1_g, ln1_b, wr, br, n_real)

    n_tok = tp + ts
    max_tiles = n_tok // MOE_TM + N_GROUPS
    src, dst, tile_group, n_used = _dispatch(x1e[:, D_MODEL:D_MODEL + 2], counts, tp, ns, n_real, max_tiles)
    y_p, y_s = _moe(x1e, src, dst, tile_group, n_used, w_gate[0].astype(BF16), w_up[0].astype(BF16),
                    w_down[0].astype(BF16), ln2_g, ln2_b, tp, ts, max_tiles)

    return (y_p.reshape(nb, seq, D_MODEL),
            y_s.reshape(ns, n_real, D_MODEL),
            st_p.reshape(1, nb, SSD_HEADS, SSD_HEAD_DIM, SSD_STATE),
            tail_p[None],
            st_s.reshape(1, ns, SSD_HEADS, SSD_HEAD_DIM, SSD_STATE),
            tail_s[None],
            v_s.reshape(ns, SUB, D_SG)[:, :n_real][None])
```

```python
import functools
import math

import jax
import jax.numpy as jnp
from jax import lax
from jax.experimental import pallas as pl
from jax.experimental.pallas import tpu as pltpu

F32 = jnp.float32
BF16 = jnp.bfloat16

D_MODEL = 4096
SSD_HEADS = 64
SSD_HEAD_DIM = 64
D_SSD = SSD_HEADS * SSD_HEAD_DIM
SSD_GROUPS = 8
SSD_STATE = 128
SSD_CONV = 4
D_BC = SSD_GROUPS * SSD_STATE
CONV_DIM = D_SSD + 2 * D_BC
GROUP_W = D_SSD // SSD_GROUPS
SG_GROUPS = 8
D_SG = 2048
SG_HEAD_DIM = D_SG // SG_GROUPS
SG_CHUNK = 128
N_GROUPS = 8
EXPERTS_PER_GROUP = 4
N_EXPERTS = N_GROUPS * EXPERTS_PER_GROUP
D_EXPERT = 512
DEPTH = 1
ALPHA = (2 * DEPTH) ** 0.25
LN_EPS = 1e-5
RMS_EPS = 1e-5

LANE = 128
SUB = 8
CHUNK = 128
HALO = 8
VMEM_LIMIT = 58 * 1024 * 1024
NEG = -1e30
META_W = LANE
MOE_TM = 256
ROW_UNROLL = 8


def _cp(sem, **kw):
    return pltpu.CompilerParams(dimension_semantics=sem, vmem_limit_bytes=VMEM_LIMIT, **kw)


def _tile(n, pref, align):
    if n <= pref:
        return n
    t = (pref // align) * align
    while t > align and n % t:
        t -= align
    assert n % t == 0, (n, pref, align)
    return t


def _silu(a):
    return a * jax.nn.sigmoid(a)


def _softplus(a):
    return jnp.maximum(a, 0.0) + jnp.log1p(jnp.exp(-jnp.abs(a)))


def _split3(v):
    hi = v.astype(BF16)
    r1 = v - hi.astype(F32)
    mid = r1.astype(BF16)
    lo = (r1 - mid.astype(F32)).astype(BF16)
    return hi, mid, lo


def _dot(a, b):
    return jnp.dot(a, b, preferred_element_type=F32)


def _dot_nt(a, b):
    return lax.dot_general(a, b, (((1,), (1,)), ((), ())), preferred_element_type=F32)


def _spread(v, m):
    hi, mid, lo = _split3(v)
    return _dot(hi, m) + _dot(mid, m) + _dot(lo, m)


def _cumsum_rows(v):
    n = v.shape[0]
    row = lax.broadcasted_iota(jnp.int32, v.shape, 0)
    s = 1
    while s < n:
        v = v + jnp.where(row >= s, pltpu.roll(v, s, axis=0), 0.0)
        s *= 2
    return v


def _proj_kernel(x_ref, w_ref, *rest, act, n_extra, cast_w):
    extras = rest[:n_extra]
    o_ref = rest[n_extra]
    if cast_w:
        wb_ref = rest[n_extra + 1]

        @pl.when(pl.program_id(1) == 0)
        def _():
            wb_ref[...] = w_ref[...].astype(BF16)
        w = wb_ref[...]
    else:
        w = w_ref[...]
    o_ref[...] = act(_dot(x_ref[...], w), *[e[...] for e in extras]).astype(o_ref.dtype)


def _proj(x, w, col0, ncols, act, out_dtype, name, bias=None, resid=None, tm_pref=1024):
    t, k = x.shape
    cast_w = w.dtype != BF16
    tm = _tile(t, tm_pref, 128)
    tn = _tile(ncols, 512 if cast_w else 1024, 128)
    assert col0 % tn == 0
    in_specs = [pl.BlockSpec((tm, k), lambda j, i: (i, 0)),
                pl.BlockSpec((k, tn), lambda j, i: (0, j + col0 // tn))]
    args = [x, w]
    if bias is not None:
        in_specs.append(pl.BlockSpec((1, tn), lambda j, i: (0, j)))
        args.append(bias)
    if resid is not None:
        in_specs.append(pl.BlockSpec((tm, tn), lambda j, i: (i, j)))
        args.append(resid)
    return pl.pallas_call(
        functools.partial(_proj_kernel, act=act, n_extra=len(args) - 2, cast_w=cast_w),
        out_shape=jax.ShapeDtypeStruct((t, ncols), out_dtype),
        grid=(ncols // tn, t // tm),
        in_specs=in_specs,
        out_specs=pl.BlockSpec((tm, tn), lambda j, i: (i, j)),
        scratch_shapes=[pltpu.VMEM((k, tn), BF16)] if cast_w else [],
        compiler_params=_cp(("arbitrary", "arbitrary")),
        name=name)(*args)


def _conv_silu(xpad_ref, cw_ref, cb_ref, lo, width, rows):
    acc = cb_ref[:, lo:lo + width]
    for k in range(SSD_CONV):
        start = HALO - (SSD_CONV - 1) + k
        acc = acc + cw_ref[k:k + 1, lo:lo + width] * xpad_ref[pl.ds(start, rows), lo:lo + width]
    return _silu(acc)


def _gated_rms(y, zs, g):
    yz = y * zs
    ms = jnp.mean(yz * yz, axis=-1, keepdims=True)
    return yz * lax.rsqrt(ms + RMS_EPS) * g


def _ssd_prompt_kernel(xbc_ref, dt_ref, zs_ref, cw_ref, cb_ref, alog_ref, dsk_ref, ng_ref, e_ref,
                       y_ref, st_ref, tail_ref, xpad_ref, stt_ref):
    c = pl.program_id(1)
    q = CHUNK

    @pl.when(c == 0)
    def _():
        xpad_ref[0:HALO, :] = jnp.zeros((HALO, CONV_DIM), F32)
        stt_ref[...] = jnp.zeros_like(stt_ref)

    xpad_ref[HALO:HALO + q, :] = xbc_ref[...]

    dtv = dt_ref[...]
    a = -jnp.exp(alog_ref[...])
    acs = _cumsum_rows(dtv * a)
    last = acs[q - 1:q, :]
    te = jnp.exp(last - acs)
    eacs = jnp.exp(acs)
    cd = jnp.broadcast_to(jnp.exp(last), (SUB, LANE))
    dsk = jnp.broadcast_to(dsk_ref[...], (SUB, LANE))
    stack3 = _split3(jnp.concatenate([dtv, te, eacs, cd, dsk], axis=0))
    acs_t = acs.T

    ii = lax.broadcasted_iota(jnp.int32, (q, q), 0)
    jj = lax.broadcasted_iota(jnp.int32, (q, q), 1)
    causal = ii >= jj
    lane = lax.broadcasted_iota(jnp.int32, (q, LANE), 1)
    first_half = lane < SSD_HEAD_DIM

    for g in range(SSD_GROUPS):
        lo = g * GROUP_W
        xs = _conv_silu(xpad_ref, cw_ref, cb_ref, lo, GROUP_W, q)
        bm = _conv_silu(xpad_ref, cw_ref, cb_ref, D_SSD + g * SSD_STATE, SSD_STATE, q)
        cm = _conv_silu(xpad_ref, cw_ref, cb_ref, D_SSD + D_BC + g * SSD_STATE, SSD_STATE, q)
        e_g = e_ref[:, lo:lo + GROUP_W]
        ex = _dot(stack3[0], e_g) + _dot(stack3[1], e_g) + _dot(stack3[2], e_g)
        dt_e, te_e, eacs_e = ex[0:q], ex[q:2 * q], ex[2 * q:3 * q]
        cd_e = ex[3 * q:3 * q + 1]
        dsk_e = ex[3 * q + SUB:3 * q + SUB + 1]
        xdt = xs * dt_e
        xdt_b = xdt.astype(BF16)
        xw_b = (xdt * te_e).astype(BF16)
        bm_b = bm.astype(BF16)
        cm_b = cm.astype(BF16)
        scores = _dot_nt(cm_b, bm_b)
        st_g = stt_ref[:, lo:lo + GROUP_W]
        y_off = _dot(cm_b, st_g.astype(BF16)) * eacs_e
        parts = []
        for k in range(GROUP_W // LANE):
            ms = []
            for hh in range(2):
                h = g * (SSD_HEADS // SSD_GROUPS) + 2 * k + hh
                seg = acs[:, h:h + 1] - acs_t[h:h + 1, :]
                ms.append((scores * jnp.exp(jnp.where(causal, seg, NEG))).astype(BF16))
            xp = xdt_b[:, k * LANE:(k + 1) * LANE]
            zero = jnp.zeros_like(xp)
            rhs = jnp.concatenate([jnp.where(first_half, xp, zero), jnp.where(first_half, zero, xp)], axis=0)
            parts.append(_dot(jnp.concatenate(ms, axis=1), rhs))
        y = jnp.concatenate(parts, axis=1) + y_off + dsk_e * xs
        zs = zs_ref[:, lo:lo + GROUP_W].astype(F32)
        y_ref[:, lo:lo + GROUP_W] = _gated_rms(y, zs, ng_ref[:, lo:lo + GROUP_W]).astype(y_ref.dtype)
        stt_ref[:, lo:lo + GROUP_W] = st_g * cd_e + _dot(bm.T.astype(BF16), xw_b)

    xpad_ref[0:HALO, :] = xpad_ref[q:q + HALO, :]

    @pl.when(c == pl.num_programs(1) - 1)
    def _():
        tail_ref[0] = xpad_ref[HALO + q - (SSD_CONV - 1):HALO + q, :]
        for g in range(SSD_GROUPS):
            lo = g * GROUP_W
            st_ref[0, lo:lo + GROUP_W, :] = stt_ref[:, lo:lo + GROUP_W].T


def _ssd_prompt(xbc, dt, zs, cw, cb, alog, dsk, ng, e_mat, nb, nc):
    t = nb * nc * CHUNK
    q = CHUNK
    row = lambda b, c: (b * nc + c, 0)
    const = lambda b, c: (0, 0)
    return pl.pallas_call(
        _ssd_prompt_kernel,
        out_shape=(jax.ShapeDtypeStruct((t, D_SSD), BF16),
                   jax.ShapeDtypeStruct((nb, D_SSD, SSD_STATE), F32),
                   jax.ShapeDtypeStruct((nb, SSD_CONV - 1, CONV_DIM), F32)),
        grid=(nb, nc),
        in_specs=[pl.BlockSpec((q, CONV_DIM), row),
                  pl.BlockSpec((q, LANE), row),
                  pl.BlockSpec((q, D_SSD), row),
                  pl.BlockSpec((SSD_CONV, CONV_DIM), const),
                  pl.BlockSpec((1, CONV_DIM), const),
                  pl.BlockSpec((1, LANE), const),
                  pl.BlockSpec((1, LANE), const),
                  pl.BlockSpec((1, D_SSD), const),
                  pl.BlockSpec((LANE, D_SSD), const)],
        out_specs=(pl.BlockSpec((q, D_SSD), row),
                   pl.BlockSpec((1, D_SSD, SSD_STATE), lambda b, c: (b, 0, 0)),
                   pl.BlockSpec((1, SSD_CONV - 1, CONV_DIM), lambda b, c: (b, 0, 0))),
        scratch_shapes=[pltpu.VMEM((HALO + q, CONV_DIM), F32),
                        pltpu.VMEM((SSD_STATE, D_SSD), F32)],
        compiler_params=_cp(("arbitrary", "arbitrary")),
        name="ssd_prompt")(xbc, dt, zs, cw, cb, alog, dsk, ng, e_mat)


def _ssd_sample_kernel(xbc_ref, dt_ref, zs_ref, buf_ref, s0_ref, cw_ref, cb_ref, alog_ref, dsk_ref,
                       ng_ref, e_ref, et_ref, g2_ref, y_ref, st_ref, tail_ref, xpad_ref,
                       *, nseq, n_real):
    r = SUB
    a = -jnp.exp(alog_ref[...])
    row = lax.broadcasted_iota(jnp.int32, (r, LANE), 0)
    real = row < n_real
    dsk = jnp.broadcast_to(dsk_ref[...], (r, LANE))

    per_seq = []
    stacks = []
    for s in range(nseq):
        xpad_ref[s, HALO - (SSD_CONV - 1):HALO, :] = buf_ref[s]
        xpad_ref[s, HALO:HALO + r, :] = xbc_ref[s * r:(s + 1) * r, :]
        tail_ref[s] = xpad_ref[s, HALO + n_real - (SSD_CONV - 1):HALO + n_real, :]
        xp_s = xpad_ref.at[s]
        xs = _conv_silu(xp_s, cw_ref, cb_ref, 0, D_SSD, r)
        bm = _conv_silu(xp_s, cw_ref, cb_ref, D_SSD, D_BC, r)
        cm = _conv_silu(xp_s, cw_ref, cb_ref, D_SSD + D_BC, D_BC, r)
        dtv = jnp.where(real, dt_ref[s * r:(s + 1) * r, :], 0.0)
        acs = _cumsum_rows(dtv * a)
        last = acs[r - 1:r, :]
        te = jnp.exp(last - acs)
        eacs = jnp.exp(acs)
        cd = jnp.exp(last)
        coefs = []
        for j in range(n_real):
            sc = _spread(cm * bm[j:j + 1, :], g2_ref[...])
            dec = jnp.where(row >= j, jnp.exp(acs - acs[j:j + 1, :]), 0.0)
            coefs.append(sc * dec)
        stacks.append(jnp.concatenate([dtv, te, eacs, dsk] + coefs, axis=0))
        per_seq.append((xs, bm, cm, cd))

    ex_all = _spread(jnp.concatenate(stacks, axis=0), e_ref[...])
    blk = (4 + n_real) * r
    for s in range(nseq):
        xs, bm, cm, cd = per_seq[s]
        ex = ex_all[s * blk:(s + 1) * blk]
        dt_e, te_e, eacs_e, dsk_e = ex[0:r], ex[r:2 * r], ex[2 * r:3 * r], ex[3 * r:4 * r]
        xdt = xs * dt_e
        xw = xdt * te_e
        y = dsk_e * xs
        for j in range(n_real):
            y = y + ex[(4 + j) * r:(5 + j) * r] * xdt[j:j + 1, :]
        cm_b = cm.astype(BF16)
        cdmat = _spread_cols(cd, et_ref)
        offs = []
        for g in range(SSD_GROUPS):
            lo = g * GROUP_W
            s_g = s0_ref[s, lo:lo + GROUP_W, :]
            offs.append(_dot_nt(cm_b[:, g * SSD_STATE:(g + 1) * SSD_STATE], s_g.astype(BF16)))
            upd = _dot(xw[:, lo:lo + GROUP_W].T, bm[:, g * SSD_STATE:(g + 1) * SSD_STATE])
            st_ref[s, lo:lo + GROUP_W, :] = s_g * cdmat[lo:lo + GROUP_W, :] + upd
        y = y + jnp.concatenate(offs, axis=1) * eacs_e
        zs = zs_ref[s * r:(s + 1) * r, :].astype(F32)
        outs = []
        for g in range(SSD_GROUPS):
            lo = g * GROUP_W
            outs.append(_gated_rms(y[:, lo:lo + GROUP_W], zs[:, lo:lo + GROUP_W], ng_ref[:, lo:lo + GROUP_W]))
        y_ref[s * r:(s + 1) * r, :] = jnp.concatenate(outs, axis=1).astype(y_ref.dtype)


def _spread_cols(v, et_ref):
    rows = jnp.broadcast_to(v, (SSD_STATE, LANE))
    hi, mid, lo = _split3(rows)
    et = et_ref[...]
    return _dot_nt(et, hi) + _dot_nt(et, mid) + _dot_nt(et, lo)


def _ssd_sample(xbc, dt, zs, buf, s0, cw, cb, alog, dsk, ng, e_mat, et_mat, g2_mat, row0, ns, n_real):
    nseq = 2
    assert ns % nseq == 0 and row0 % (nseq * SUB) == 0
    rb = row0 // (nseq * SUB)
    row = lambda i: (rb + i, 0)
    const = lambda i: (0, 0)
    seq3 = lambda i: (i, 0, 0)
    return pl.pallas_call(
        functools.partial(_ssd_sample_kernel, nseq=nseq, n_real=n_real),
        out_shape=(jax.ShapeDtypeStruct((ns * SUB, D_SSD), BF16),
                   jax.ShapeDtypeStruct((ns, D_SSD, SSD_STATE), F32),
                   jax.ShapeDtypeStruct((ns, SSD_CONV - 1, CONV_DIM), F32)),
        grid=(ns // nseq,),
        in_specs=[pl.BlockSpec((nseq * SUB, CONV_DIM), row),
                  pl.BlockSpec((nseq * SUB, LANE), row),
                  pl.BlockSpec((nseq * SUB, D_SSD), row),
                  pl.BlockSpec((nseq, SSD_CONV - 1, CONV_DIM), seq3),
                  pl.BlockSpec((nseq, D_SSD, SSD_STATE), seq3),
                  pl.BlockSpec((SSD_CONV, CONV_DIM), const),
                  pl.BlockSpec((1, CONV_DIM), const),
                  pl.BlockSpec((1, LANE), const),
                  pl.BlockSpec((1, LANE), const),
                  pl.BlockSpec((1, D_SSD), const),
                  pl.BlockSpec((LANE, D_SSD), const),
                  pl.BlockSpec((D_SSD, LANE), const),
                  pl.BlockSpec((D_BC, LANE), const)],
        out_specs=(pl.BlockSpec((nseq * SUB, D_SSD), lambda i: (i, 0)),
                   pl.BlockSpec((nseq, D_SSD, SSD_STATE), seq3),
                   pl.BlockSpec((nseq, SSD_CONV - 1, CONV_DIM), seq3)),
        scratch_shapes=[pltpu.VMEM((nseq, HALO + SUB, CONV_DIM), F32)],
        compiler_params=_cp(("arbitrary",)),
        name="ssd_sample")(xbc, dt, zs, buf, s0, cw, cb, alog, dsk, ng, e_mat, et_mat, g2_mat)


def _sgu_kernel(u_ref, gv_ref, lg_ref, lb_ref, w_ref, bt_ref, y_ref, *v_out, period):
    gv = gv_ref[...]
    mu = jnp.mean(gv, axis=-1, keepdims=True)
    d = gv - mu
    var = jnp.mean(d * d, axis=-1, keepdims=True)
    v = d * lax.rsqrt(var + LN_EPS) * lg_ref[...] + lb_ref[...]
    if v_out:
        v_out[0][...] = v
    v_b = v.astype(BF16)
    n = SG_CHUNK
    ii = lax.broadcasted_iota(jnp.int32, (n, n), 0)
    jj = lax.broadcasted_iota(jnp.int32, (n, n), 1)
    keep = (jj <= ii) & ((ii // period) == (jj // period))
    for g in range(SG_GROUPS):
        lo = g * SG_HEAD_DIM
        w = jnp.where(keep, w_ref[g], 0.0).astype(BF16)
        mixed = _dot(w, v_b[:, lo:lo + SG_HEAD_DIM]) + bt_ref[:, g:g + 1]
        y_ref[:, lo:lo + SG_HEAD_DIM] = (u_ref[:, lo:lo + SG_HEAD_DIM].astype(F32) * mixed).astype(y_ref.dtype)


def _sgu(u, gv, lg, lb, w, bt, row0, nrows, period, want_v):
    n = SG_CHUNK
    rb = row0 // n
    row = lambda i: (rb + i, 0)
    own = lambda i: (i, 0)
    const = lambda i: (0, 0)
    out_shape = [jax.ShapeDtypeStruct((nrows, D_SG), BF16)]
    out_specs = [pl.BlockSpec((n, D_SG), own)]
    if want_v:
        out_shape.append(jax.ShapeDtypeStruct((nrows, D_SG), F32))
        out_specs.append(pl.BlockSpec((n, D_SG), own))
    return pl.pallas_call(
        functools.partial(_sgu_kernel, period=period),
        out_shape=tuple(out_shape),
        grid=(nrows // n,),
        in_specs=[pl.BlockSpec((n, D_SG), row),
                  pl.BlockSpec((n, D_SG), row),
                  pl.BlockSpec((1, D_SG), const),
                  pl.BlockSpec((1, D_SG), const),
                  pl.BlockSpec((SG_GROUPS, n, n), lambda i: (0, 0, 0)),
                  pl.BlockSpec((n, SG_GROUPS), const)],
        out_specs=tuple(out_specs),
        compiler_params=_cp(("arbitrary",)),
        name="sgu_sample" if want_v else "sgu_prompt")(u, gv, lg, lb, w, bt)


def _merge_kernel(ya_ref, yb_ref, pa_ref, pb_ref, ga_ref, gb_ref, o_ref):
    a = _dot(ya_ref[...], pa_ref[...])
    b = _dot(yb_ref[...], pb_ref[...])
    o_ref[...] = (ga_ref[...].astype(F32) * a + gb_ref[...].astype(F32) * b).astype(o_ref.dtype)


def _merge(ya, yb, p_ssd, p_sg, gates, row0, name):
    n = ya.shape[0]
    tm = _tile(n, 512, 128)
    assert row0 % tm == 0
    rb = row0 // tm
    tn = 1024
    nb = D_MODEL // tn
    return pl.pallas_call(
        _merge_kernel,
        out_shape=jax.ShapeDtypeStruct((n, D_MODEL), BF16),
        grid=(nb, n // tm),
        in_specs=[pl.BlockSpec((tm, D_SSD), lambda j, i: (i, 0)),
                  pl.BlockSpec((tm, D_SG), lambda j, i: (i, 0)),
                  pl.BlockSpec((D_SSD, tn), lambda j, i: (0, j)),
                  pl.BlockSpec((D_SG, tn), lambda j, i: (0, j)),
                  pl.BlockSpec((tm, tn), lambda j, i: (rb + i, j)),
                  pl.BlockSpec((tm, tn), lambda j, i: (rb + i, j + nb))],
        out_specs=pl.BlockSpec((tm, tn), lambda j, i: (i, j)),
        compiler_params=_cp(("arbitrary", "arbitrary")),
        name=name)(ya, yb, p_ssd, p_sg, gates, gates)


def _layer_norm(v, g, b):
    mu = jnp.mean(v, axis=-1, keepdims=True)
    d = v - mu
    var = jnp.mean(d * d, axis=-1, keepdims=True)
    return d * lax.rsqrt(var + LN_EPS) * g + b


def _route(logits, valid, carry_ref):
    tm = logits.shape[0]
    lane = lax.broadcasted_iota(jnp.int32, (tm, LANE), 1).astype(F32)
    big = float(LANE)
    is_g = lane < N_GROUPS
    gl = jnp.where(is_g, logits, -jnp.inf)
    gmax = jnp.max(gl, axis=-1, keepdims=True)
    grp = jnp.min(jnp.where(gl == gmax, lane, big), axis=-1, keepdims=True)
    p_grp = 1.0 / jnp.sum(jnp.where(is_g, jnp.exp(gl - gmax), 0.0), axis=-1, keepdims=True)
    e0 = N_GROUPS + grp * EXPERTS_PER_GROUP
    in_grp = (lane >= e0) & (lane < e0 + EXPERTS_PER_GROUP)
    el = jnp.where(in_grp, logits, -jnp.inf)
    t1 = jnp.max(el, axis=-1, keepdims=True)
    i1 = jnp.min(jnp.where(el == t1, lane, big), axis=-1, keepdims=True)
    el2 = jnp.where(lane == i1, -jnp.inf, el)
    t2 = jnp.max(el2, axis=-1, keepdims=True)
    i2 = jnp.min(jnp.where(el2 == t2, lane, big), axis=-1, keepdims=True)
    ex = jnp.exp(t2 - t1)
    w1 = p_grp / (1.0 + ex)
    w2 = p_grp * ex / (1.0 + ex)
    onehot = jnp.where((lane == grp) & valid, 1.0, 0.0)
    ri = lax.broadcasted_iota(jnp.int32, (tm, tm), 0)
    ci = lax.broadcasted_iota(jnp.int32, (tm, tm), 1)
    before = jnp.where(ci < ri, 1.0, 0.0).astype(BF16)
    prefix = _dot(before, onehot.astype(BF16)) + carry_ref[...]
    rank = jnp.sum(prefix * onehot, axis=-1, keepdims=True)
    carry_ref[...] = carry_ref[...] + jnp.sum(onehot, axis=0, keepdims=True)
    meta = jnp.where(lane == 0, jnp.where(valid, grp, -1.0), 0.0)
    meta = jnp.where(lane == 1, rank, meta)
    for k in range(EXPERTS_PER_GROUP):
        gk = jnp.where(i1 == e0 + k, w1, jnp.where(i2 == e0 + k, w2, 0.0))
        meta = jnp.where(lane == 2 + k, gk, meta)
    return meta


def _ln_route_kernel(rp_ref, rs_ref, g_ref, b_ref, wr_ref, br_ref, o_ref, cnt_ref, *, n_prompt_tiles, n_real):
    i = pl.program_id(0)

    @pl.when(i == 0)
    def _():
        cnt_ref[...] = jnp.zeros_like(cnt_ref)

    def body(r_ref, is_prompt):
        x1 = _layer_norm(r_ref[...], g_ref[...], b_ref[...])
        o_ref[:, 0:D_MODEL] = x1
        xh = x1.astype(BF16)
        xl = (x1 - xh.astype(F32)).astype(BF16)
        wr = wr_ref[...]
        wh = wr.astype(BF16)
        wl = (wr - wh.astype(F32)).astype(BF16)
        logits = _dot(xh, wh) + _dot(xh, wl) + _dot(xl, wh) + br_ref[...]
        tm = x1.shape[0]
        row = lax.broadcasted_iota(jnp.int32, (tm, LANE), 0)
        valid = (row >= 0) if is_prompt else ((row % SUB) < n_real)
        o_ref[:, D_MODEL:D_MODEL + META_W] = _route(logits, valid, cnt_ref)

    @pl.when(i < n_prompt_tiles)
    def _():
        body(rp_ref, True)

    @pl.when(i >= n_prompt_tiles)
    def _():
        body(rs_ref, False)


def _ln_route(r_p, r_s, g, b, wr, br, n_real):
    tp, tsp = r_p.shape[0], r_s.shape[0]
    tm = _tile(math.gcd(tp, tsp), 256, 128)
    npt = tp // tm
    const = lambda i: (0, 0)
    return pl.pallas_call(
        functools.partial(_ln_route_kernel, n_prompt_tiles=npt, n_real=n_real),
        out_shape=(jax.ShapeDtypeStruct((tp + tsp, D_MODEL + META_W), F32),
                   jax.ShapeDtypeStruct((1, LANE), F32)),
        grid=((tp + tsp) // tm,),
        in_specs=[pl.BlockSpec((tm, D_MODEL), lambda i: (jnp.minimum(i, npt - 1), 0)),
                  pl.BlockSpec((tm, D_MODEL), lambda i: (jnp.maximum(i - npt, 0), 0)),
                  pl.BlockSpec((1, D_MODEL), const),
                  pl.BlockSpec((1, D_MODEL), const),
                  pl.BlockSpec((D_MODEL, LANE), const),
                  pl.BlockSpec((1, LANE), const)],
        out_specs=(pl.BlockSpec((tm, D_MODEL + META_W), lambda i: (i, 0)),
                   pl.BlockSpec((1, LANE), const)),
        compiler_params=_cp(("arbitrary",)),
        name="ln_route")(r_p, r_s, g, b, wr, br)


def _moe_kernel(src_ref, dst_ref, tgrp_ref, nvalid_ref, nused_ref, x_hbm, wg_ref, wu_ref, wd_ref, g_ref, b_ref,
                yp_hbm, ys_hbm, xg_ref, xb_ref, acc_ref, ob_ref, gsem, ssem, *, tm, n_prompt):
    t = pl.program_id(0)
    e = pl.program_id(1)
    n_used = nused_ref[0]
    slot = t % 2

    def gather_start(tile, sl):
        def body(k, carry):
            for u in range(ROW_UNROLL):
                r = k * ROW_UNROLL + u
                s = src_ref[tile * tm + r]
                pltpu.make_async_copy(x_hbm.at[pl.ds(s, 1)], xg_ref.at[sl, pl.ds(r, 1)], gsem.at[sl]).start()
            return carry
        lax.fori_loop(0, tm // ROW_UNROLL, body, 0)

    def gather_wait(sl):
        pltpu.make_async_copy(x_hbm.at[pl.ds(0, tm)], xg_ref.at[sl], gsem.at[sl]).wait()

    def scatter_start(tile):
        def body(k, carry):
            for u in range(ROW_UNROLL):
                r = k * ROW_UNROLL + u
                d = dst_ref[tile * tm + r]

                @pl.when((d >= 0) & (d < n_prompt))
                def _():
                    pltpu.make_async_copy(ob_ref.at[pl.ds(r, 1)], yp_hbm.at[pl.ds(d, 1)], ssem.at[0]).start()

                @pl.when(d >= n_prompt)
                def _():
                    pltpu.make_async_copy(ob_ref.at[pl.ds(r, 1)], ys_hbm.at[pl.ds(d - n_prompt, 1)],
                                          ssem.at[0]).start()
            return carry
        lax.fori_loop(0, tm // ROW_UNROLL, body, 0)

    def scatter_wait(tile):
        nv = nvalid_ref[tile]
        nv_al = pl.multiple_of((nv // SUB) * SUB, SUB)

        @pl.when(nv_al > 0)
        def _():
            pltpu.make_async_copy(ob_ref.at[pl.ds(0, nv_al)], yp_hbm.at[pl.ds(0, nv_al)], ssem.at[0]).wait()

        p = SUB // 2
        while p:
            @pl.when((nv & p) != 0)
            def _():
                pltpu.make_async_copy(ob_ref.at[pl.ds(0, p)], yp_hbm.at[pl.ds(0, p)], ssem.at[0]).wait()
            p //= 2

    @pl.when(t < n_used)
    def _():
        @pl.when(e == 0)
        def _():
            @pl.when(t == 0)
            def _():
                gather_start(0, 0)
            gather_wait(slot)

            @pl.when(t + 1 < n_used)
            def _():
                gather_start(t + 1, 1 - slot)
            xb_ref[...] = xg_ref[slot, :, 0:D_MODEL].astype(BF16)

        xb = xb_ref[...]
        lane = lax.broadcasted_iota(jnp.int32, (tm, META_W), 1)
        gate = jnp.sum(jnp.where(lane == 2 + e, xg_ref[slot, :, D_MODEL:D_MODEL + META_W], 0.0),
                       axis=-1, keepdims=True)
        hg = _dot(xb, wg_ref[0])
        hu = _dot(xb, wu_ref[0])
        hh = (_silu(hg) * hu * gate).astype(BF16)
        part = _dot(hh, wd_ref[0])

        @pl.when(e == 0)
        def _():
            acc_ref[...] = part

        @pl.when(e > 0)
        def _():
            acc_ref[...] = acc_ref[...] + part

        @pl.when(e == EXPERTS_PER_GROUP - 1)
        def _():
            @pl.when(t > 0)
            def _():
                scatter_wait(t - 1)
            ob_ref[...] = _layer_norm(ALPHA * xg_ref[slot, :, 0:D_MODEL] + acc_ref[...], g_ref[...], b_ref[...])
            scatter_start(t)

            @pl.when(t == n_used - 1)
            def _():
                scatter_wait(t)


def _moe(x1e, src, dst, tile_group, n_valid, n_used, wg, wu, wd, g, b, n_prompt, n_sample, max_tiles):
    tm = MOE_TM

    def wmap(t, e, src, dst, tg, nv, nu):
        last = tg[nu[0] - 1] * EXPERTS_PER_GROUP + EXPERTS_PER_GROUP - 1
        return (jnp.where(t < nu[0], tg[t] * EXPERTS_PER_GROUP + e, last), 0, 0)

    const = lambda t, e, src, dst, tg, nv, nu: (0, 0)
    grid_spec = pltpu.PrefetchScalarGridSpec(
        num_scalar_prefetch=5,
        grid=(max_tiles, EXPERTS_PER_GROUP),
        in_specs=[pl.BlockSpec(memory_space=pl.ANY),
                  pl.BlockSpec((1, D_MODEL, D_EXPERT), wmap),
                  pl.BlockSpec((1, D_MODEL, D_EXPERT), wmap),
                  pl.BlockSpec((1, D_EXPERT, D_MODEL), wmap),
                  pl.BlockSpec((1, D_MODEL), const),
                  pl.BlockSpec((1, D_MODEL), const)],
        out_specs=(pl.BlockSpec(memory_space=pl.ANY), pl.BlockSpec(memory_space=pl.ANY)),
        scratch_shapes=[pltpu.VMEM((2, tm, D_MODEL + META_W), F32),
                        pltpu.VMEM((tm, D_MODEL), BF16),
                        pltpu.VMEM((tm, D_MODEL), F32),
                        pltpu.VMEM((tm, D_MODEL), F32),
                        pltpu.SemaphoreType.DMA((2,)),
                        pltpu.SemaphoreType.DMA((1,))])
    return pl.pallas_call(
        functools.partial(_moe_kernel, tm=tm, n_prompt=n_prompt),
        out_shape=(jax.ShapeDtypeStruct((n_prompt, D_MODEL), F32),
                   jax.ShapeDtypeStruct((n_sample, D_MODEL), F32)),
        grid_spec=grid_spec,
        compiler_params=_cp(("arbitrary", "arbitrary"), has_side_effects=True, disable_bounds_checks=True),
        name="moe_ln2")(src, dst, tile_group, n_valid, n_used, x1e, wg, wu, wd, g, b)


def _dispatch(meta, counts, n_prompt, ns, n_real, max_tiles):
    tm = MOE_TM
    t = meta.shape[0]
    grp = meta[:, 0].astype(jnp.int32)
    rank = meta[:, 1].astype(jnp.int32)
    cnt = counts[0, :N_GROUPS].astype(jnp.int32)
    tiles = (cnt + tm - 1) // tm
    tile_end = jnp.cumsum(tiles)
    row_start = (tile_end - tiles) * tm
    valid = grp >= 0
    pos = jnp.where(valid, row_start[jnp.maximum(grp, 0)] + rank, max_tiles * tm)
    rows = jnp.arange(t, dtype=jnp.int32)
    src = jnp.full((max_tiles * tm,), -1, jnp.int32).at[pos].set(rows, mode="drop")
    rel = src - n_prompt
    dst = jnp.where(src < n_prompt, src, n_prompt + (rel // SUB) * n_real + rel % SUB)
    dst = jnp.where(src < 0, -1, dst)
    src = jnp.maximum(src, 0)
    tile_ids = jnp.arange(max_tiles, dtype=jnp.int32)
    tile_group = jnp.minimum(jnp.sum((tile_end[None, :] <= tile_ids[:, None]).astype(jnp.int32), axis=1),
                             N_GROUPS - 1)
    n_valid = jnp.clip(row_start[tile_group] + cnt[tile_group] - tile_ids * tm, 0, tm)
    return src, dst, tile_group, n_valid, tile_end[-1:].astype(jnp.int32)


def _head_spread_matrices():
    h = lax.broadcasted_iota(jnp.int32, (LANE, D_SSD), 0)
    c = lax.broadcasted_iota(jnp.int32, (LANE, D_SSD), 1)
    e_mat = (c // SSD_HEAD_DIM == h).astype(BF16)
    n = lax.broadcasted_iota(jnp.int32, (D_BC, LANE), 0)
    hh = lax.broadcasted_iota(jnp.int32, (D_BC, LANE), 1)
    g2 = ((n // SSD_STATE == hh // (SSD_HEADS // SSD_GROUPS)) & (hh < SSD_HEADS)).astype(BF16)
    return e_mat, e_mat.T, g2


def _pad_lanes(v, width=LANE):
    return jnp.pad(v, ((0, 0), (0, width - v.shape[-1])))


def kernel(x_prompt, x_sample, state_ssd, state_ssd_conv, w_in, conv_w, conv_b, dt_bias, a_log, d_skip, ssd_norm_g, sg_ln_g, sg_ln_b, sg_w, sg_b, p_ssd, p_sg, w_out, ln1_g, ln1_b, w_router_group, b_router_group, w_router_expert, b_router_expert, w_gate, w_up, w_down, ln2_g, ln2_b):
    nb, seq, _ = x_prompt.shape
    ns, n_real, _ = x_sample.shape
    assert seq % CHUNK == 0 and n_real <= SUB and SSD_CONV - 1 <= n_real
    tp = nb * seq
    tsp = ns * SUB
    ts = ns * n_real
    assert tsp % CHUNK == 0
    nc = seq // CHUNK

    xp = x_prompt.reshape(tp, D_MODEL)
    xs_pad = jnp.pad(x_sample, ((0, 0), (0, SUB - n_real), (0, 0))).reshape(tsp, D_MODEL)
    x_all = jnp.concatenate([xp.astype(BF16), xs_pad.astype(BF16)], axis=0)

    w = w_in[0]
    c_dt = D_SSD + CONV_DIM
    w_tail = w[:, c_dt + SSD_HEADS:].astype(BF16)

    zs = _proj(x_all, w, 0, D_SSD, _silu, BF16, "proj_z")
    xbc = _proj(x_all, w, D_SSD, CONV_DIM, lambda a: a, F32, "proj_xbc")
    dt = _proj(x_all, w, c_dt, LANE, lambda a, bias: _softplus(a + bias), F32, "proj_dt",
               bias=_pad_lanes(dt_bias))
    u = _proj(x_all, w_tail, 0, D_SG, jax.nn.gelu, BF16, "proj_u")
    gv = _proj(x_all, w_tail, D_SG, D_SG, jax.nn.gelu, F32, "proj_v")
    gates = _proj(x_all, w_tail, 2 * D_SG, 2 * D_MODEL, jax.nn.sigmoid, BF16, "proj_gates")

    e_mat, et_mat, g2_mat = _head_spread_matrices()
    alog = _pad_lanes(a_log)
    dsk = _pad_lanes(d_skip)
    ya_p, st_p, tail_p = _ssd_prompt(xbc, dt, zs, conv_w[0], conv_b, alog, dsk, ssd_norm_g, e_mat, nb, nc)
    ya_s, st_s, tail_s = _ssd_sample(xbc, dt, zs, state_ssd_conv[0],
                                     state_ssd[0].reshape(ns, D_SSD, SSD_STATE), conv_w[0], conv_b,
                                     alog, dsk, ssd_norm_g, e_mat, et_mat, g2_mat, tp, ns, n_real)

    bt_p = sg_b[0].T
    (yb_p,) = _sgu(u, gv, sg_ln_g, sg_ln_b, sg_w[0], bt_p, 0, tp, SG_CHUNK, False)
    reps = SG_CHUNK // SUB
    w_s = jnp.tile(sg_w[0][:, :SUB, :SUB], (1, reps, reps))
    bt_s = jnp.tile(sg_b[0][:, :SUB].T, (reps, 1))
    yb_s, v_s = _sgu(u, gv, sg_ln_g, sg_ln_b, w_s, bt_s, tp, tsp, SUB, True)

    pa, pb = p_ssd[0].astype(BF16), p_sg[0].astype(BF16)
    m_p = _merge(ya_p, yb_p, pa, pb, gates, 0, "merge_prompt")
    m_s = _merge(ya_s, yb_s, pa, pb, gates, tp, "merge_sample")

    resid = lambda acc, x: ALPHA * x + acc
    r_p = _proj(m_p, w_out[0], 0, D_MODEL, resid, F32, "outproj_prompt", resid=xp)
    r_s = _proj(m_s, w_out[0], 0, D_MODEL, resid, F32, "outproj_sample", resid=xs_pad)

    wr = _pad_lanes(jnp.concatenate([w_router_group[0], w_router_expert[0]], axis=1))
    br = _pad_lanes(jnp.concatenate([b_router_group, b_router_expert], axis=1))
    x1e, counts = _ln_route(r_p, r_s, ln1_g, ln1_b, wr, br, n_real)

    n_tok = tp + ts
    max_tiles = n_tok // MOE_TM + N_GROUPS
    src, dst, tile_group, n_valid, n_used = _dispatch(x1e[:, D_MODEL:D_MODEL + 2], counts, tp, ns, n_real,
                                                      max_tiles)
    y_p, y_s = _moe(x1e, src, dst, tile_group, n_valid, n_used, w_gate[0].astype(BF16), w_up[0].astype(BF16),
                    w_down[0].astype(BF16), ln2_g, ln2_b, tp, ts, max_tiles)

    return (y_p.reshape(nb, seq, D_MODEL),
            y_s.reshape(ns, n_real, D_MODEL),
            st_p.reshape(1, nb, SSD_HEADS, SSD_HEAD_DIM, SSD_STATE),
            tail_p[None],
            st_s.reshape(1, ns, SSD_HEADS, SSD_HEAD_DIM, SSD_STATE),
            tail_s[None],
            v_s.reshape(ns, SUB, D_SG)[:, :n_real][None])
```
